```python
import jax, jax.numpy as jnp
from jax import lax
import numpy as np

D_MODEL = 2048
BATCH = 1
SEQ = 8192
DEPTH = 4

N_MIXERS = 2
N_CONV_LAYERS = (DEPTH + 1) // 2
N_DELTA_LAYERS = DEPTH // 2
PLE_DIM = 256
EPS = 1e-6
CONV_WIDTH = 2 * D_MODEL
CONV_KERNEL = 31
HEAD_K_DIM = 128
HEAD_V_DIM = 128
N_K_HEADS = D_MODEL // 128
N_V_HEADS = 2 * N_K_HEADS
KEY_DIM = N_K_HEADS * HEAD_K_DIM
VALUE_DIM = N_V_HEADS * HEAD_V_DIM
QKV_DIM = 2 * KEY_DIM + VALUE_DIM
DELTA_PROJ = QKV_DIM + VALUE_DIM + 2 * N_V_HEADS
SHORT_CONV = 4
CHUNK = 64

kernel_name = "hybrid_conformer_conv_gated_deltanet_ple"


def rms_norm(x, g):
    xf = x.astype(jnp.float32)
    y = xf * lax.rsqrt(jnp.mean(xf * xf, axis=-1, keepdims=True) + EPS)
    return (y * g.astype(jnp.float32)).astype(x.dtype)


def layer_norm(x, g, b):
    xf = x.astype(jnp.float32)
    mu = jnp.mean(xf, axis=-1, keepdims=True)
    xc = xf - mu
    y = xc * lax.rsqrt(jnp.mean(xc * xc, axis=-1, keepdims=True) + EPS)
    return (y * g.astype(jnp.float32) + b.astype(jnp.float32)).astype(x.dtype)


def l2_norm(x):
    return x * lax.rsqrt(jnp.sum(x * x, axis=-1, keepdims=True) + EPS)


def causal_depthwise_conv(x, w):
    K = w.shape[0]
    return lax.conv_general_dilated(
        x, w[:, None, :].astype(x.dtype), window_strides=(1,), padding=[(K - 1, 0)],
        dimension_numbers=("NWC", "WIO", "NWC"), feature_group_count=x.shape[-1])


def conformer_conv_mixer(h, w_in, dw_w, dw_b, ln_g, ln_b, w_out):
    val, glu_gate, z = jnp.split(h @ w_in, 3, axis=-1)
    u = val * jax.nn.sigmoid(glu_gate)
    u = causal_depthwise_conv(u, dw_w) + dw_b.astype(u.dtype)
    u = layer_norm(u, ln_g, ln_b)
    u = jax.nn.silu(u) * jax.nn.silu(z)
    return u @ w_out


def chunk_gated_delta_rule(q, k, v, g, beta):
    B, S, H, dk = q.shape
    dv = v.shape[-1]
    N = S // CHUNK
    f32 = jnp.float32
    q = l2_norm(q.astype(f32)) * (dk ** -0.5)
    k = l2_norm(k.astype(f32))
    v = v.astype(f32)

    def to_chunks(t):
        return t.reshape(B, N, CHUNK, H, -1).transpose(0, 3, 1, 2, 4)

    q, k, v = to_chunks(q), to_chunks(k), to_chunks(v)
    beta = beta.astype(f32).reshape(B, N, CHUNK, H).transpose(0, 3, 1, 2)
    g = jnp.cumsum(g.astype(f32).reshape(B, N, CHUNK, H).transpose(0, 3, 1, 2), axis=-1)

    causal = jnp.tril(jnp.ones((CHUNK, CHUNK), dtype=bool))
    strict = jnp.tril(jnp.ones((CHUNK, CHUNK), dtype=bool), k=-1)
    diff = g[..., :, None] - g[..., None, :]
    decay = jnp.exp(jnp.where(causal, diff, -jnp.inf))

    kb = k * beta[..., None]
    L = jnp.where(strict, jnp.einsum('bhncd,bhnsd->bhncs', kb, k) * decay, 0.0)
    eye = jnp.eye(CHUNK, dtype=f32)
    T = lax.linalg.triangular_solve(eye + L, jnp.broadcast_to(eye, L.shape),
                                    left_side=True, lower=True, unit_diagonal=True)
    u = jnp.einsum('bhncs,bhnsd->bhncd', T, v * beta[..., None])
    w = jnp.einsum('bhncs,bhnsd->bhncd', T, kb * jnp.exp(g)[..., None])
    qk = jnp.einsum('bhncd,bhnsd->bhncs', q, k) * decay
    g_last = g[..., -1]
    q_dec = q * jnp.exp(g)[..., None]
    k_dec = k * jnp.exp(g_last[..., None] - g)[..., None]

    def step(state, inp):
        q_c, k_c, u_c, w_c, qk_c, gl = inp
        v_new = u_c - jnp.einsum('bhcd,bhde->bhce', w_c, state)
        o = jnp.einsum('bhcd,bhde->bhce', q_c, state) + jnp.einsum('bhcs,bhse->bhce', qk_c, v_new)
        state = state * jnp.exp(gl)[..., None, None] + jnp.einsum('bhcd,bhce->bhde', k_c, v_new)
        return state, o

    xs = tuple(jnp.moveaxis(t, 2, 0) for t in (q_dec, k_dec, u, w, qk, g_last))
    _, o = lax.scan(step, jnp.zeros((B, H, dk, dv), f32), xs)
    return o.transpose(1, 0, 3, 2, 4).reshape(B, S, H, dv)


def gated_deltanet_mixer(h, w_in, conv_w, a_log, dt_bias, out_norm_g, w_out):
    B, S, _ = h.shape
    proj = h @ w_in
    qkv, z, b, a = jnp.split(proj, [QKV_DIM, QKV_DIM + VALUE_DIM, QKV_DIM + VALUE_DIM + N_V_HEADS], axis=-1)
    qkv = jax.nn.silu(causal_depthwise_conv(qkv, conv_w))
    q, k, v = jnp.split(qkv, [KEY_DIM, 2 * KEY_DIM], axis=-1)
    rep = N_V_HEADS // N_K_HEADS
    q = jnp.repeat(q.reshape(B, S, N_K_HEADS, HEAD_K_DIM), rep, axis=2)
    k = jnp.repeat(k.reshape(B, S, N_K_HEADS, HEAD_K_DIM), rep, axis=2)
    v = v.reshape(B, S, N_V_HEADS, HEAD_V_DIM)
    beta = jax.nn.sigmoid(b.astype(jnp.float32))
    g = -jnp.exp(a_log.astype(jnp.float32)) * jax.nn.softplus(a.astype(jnp.float32) + dt_bias.astype(jnp.float32))
    o = chunk_gated_delta_rule(q, k, v, g, beta)
    zf = z.reshape(B, S, N_V_HEADS, HEAD_V_DIM).astype(jnp.float32)
    o = rms_norm(o, out_norm_g) * jax.nn.silu(zf)
    return o.reshape(B, S, VALUE_DIM).astype(h.dtype) @ w_out


def per_layer_embedding(x, p_i, gate_w, proj_w):
    gate = jax.nn.sigmoid((x @ gate_w).astype(jnp.float32))
    return x + (gate * (p_i @ proj_w).astype(jnp.float32)).astype(x.dtype)


def setup_inputs(seed: int = 0) -> dict:
    key = jax.random.key(seed)
    ks = jax.random.split(key, 24)
    f32 = jnp.float32
    nA, nB, D, E = N_CONV_LAYERS, N_DELTA_LAYERS, D_MODEL, CONV_WIDTH
    nrm = lambda k, shape, fan: jax.random.normal(k, shape, f32) * (fan ** -0.5)
    gain = lambda k, shape: 1.0 + 0.02 * jax.random.normal(k, shape, f32)
    dt = jnp.exp(jax.random.uniform(ks[12], (nB, N_V_HEADS), f32, np.log(1e-3), np.log(1e-1)))
    return {
        "x": jax.random.normal(ks[0], (BATCH, SEQ, D), f32),
        "p": jax.random.normal(ks[1], (DEPTH, BATCH, SEQ, PLE_DIM), f32),
        "conv_norm_g": gain(ks[2], (nA, D)),
        "conv_w_in": nrm(ks[3], (nA, D, 3 * E), D),
        "conv_dw_w": nrm(ks[4], (nA, CONV_KERNEL, E), CONV_KERNEL),
        "conv_dw_b": 0.02 * jax.random.normal(ks[5], (nA, E), f32),
        "conv_ln_g": gain(ks[6], (nA, E)),
        "conv_ln_b": 0.02 * jax.random.normal(ks[7], (nA, E), f32),
        "conv_w_out": nrm(ks[8], (nA, E, D), E),
        "delta_norm_g": gain(ks[9], (nB, D)),
        "delta_w_in": nrm(ks[10], (nB, D, DELTA_PROJ), D),
        "delta_conv_w": nrm(ks[11], (nB, SHORT_CONV, QKV_DIM), SHORT_CONV),
        "delta_a_log": jnp.log(jax.random.uniform(ks[13], (nB, N_V_HEADS), f32, 1.0, 16.0)),
        "delta_dt_bias": dt + jnp.log(-jnp.expm1(-dt)),
        "delta_out_norm_g": gain(ks[14], (nB, HEAD_V_DIM)),
        "delta_w_out": nrm(ks[15], (nB, VALUE_DIM, D), VALUE_DIM),
        "ple_gate_w": nrm(ks[16], (DEPTH, D, D), D),
        "ple_proj_w": nrm(ks[17], (DEPTH, PLE_DIM, D), PLE_DIM),
        "final_norm_g": gain(ks[18], (D,)),
    }


def reference(x, p, conv_norm_g, conv_w_in, conv_dw_w, conv_dw_b, conv_ln_g, conv_ln_b, conv_w_out,
              delta_norm_g, delta_w_in, delta_conv_w, delta_a_log, delta_dt_bias, delta_out_norm_g,
              delta_w_out, ple_gate_w, ple_proj_w, final_norm_g):
    for i in range(DEPTH):
        j = i // N_MIXERS
        if i % N_MIXERS == 0:
            x = x + conformer_conv_mixer(rms_norm(x, conv_norm_g[j]), conv_w_in[j], conv_dw_w[j],
                                         conv_dw_b[j], conv_ln_g[j], conv_ln_b[j], conv_w_out[j])
        else:
            x = x + gated_deltanet_mixer(rms_norm(x, delta_norm_g[j]), delta_w_in[j], delta_conv_w[j],
                                         delta_a_log[j], delta_dt_bias[j], delta_out_norm_g[j], delta_w_out[j])
        x = per_layer_embedding(x, p[i], ple_gate_w[i], ple_proj_w[i])
    return rms_norm(x, final_norm_g)
```

```python
import functools

import jax
import jax.numpy as jnp
from jax import lax
from jax.experimental import pallas as pl
from jax.experimental.pallas import tpu as pltpu

F32 = jnp.float32
BF16 = jnp.bfloat16

SEQ = 8192
D_MODEL = 2048
DEPTH = 4
PLE_DIM = 256
EPS = 1e-6
CONV_WIDTH = 2 * D_MODEL
CONV_KERNEL = 31
HEAD_DIM = 128
N_K_HEADS = 16
N_V_HEADS = 32
KEY_DIM = N_K_HEADS * HEAD_DIM
VALUE_DIM = N_V_HEADS * HEAD_DIM
QKV_DIM = 2 * KEY_DIM + VALUE_DIM
SHORT_CONV = 4
CHUNK = 64
N_CHUNKS = SEQ // CHUNK

LANES = 128
VMEM_LIMIT = 56 * 1024 * 1024

CONV_HALO = 32
SHORT_HALO = 16


def _sigmoid(x):
    return 1.0 / (1.0 + jnp.exp(-x))


def _silu(x):
    return x * _sigmoid(x)


def _params(*sem):
    return pltpu.CompilerParams(dimension_semantics=sem, vmem_limit_bytes=VMEM_LIMIT)


def _resident(shape):
    nd = len(shape)
    return pl.BlockSpec(shape, lambda *_: (0,) * nd, pipeline_mode=pl.Buffered(1))


def _rmsnorm_kernel(x_ref, g_ref, h_ref):
    x = x_ref[...]
    ms = jnp.mean(x * x, axis=-1, keepdims=True)
    h_ref[...] = (x * lax.rsqrt(ms + EPS) * g_ref[...]).astype(h_ref.dtype)


def _rmsnorm(x, g, tm=512):
    return pl.pallas_call(
        _rmsnorm_kernel,
        grid=(SEQ // tm,),
        in_specs=[pl.BlockSpec((tm, D_MODEL), lambda i: (i, 0)),
                  pl.BlockSpec((1, D_MODEL), lambda i: (0, 0))],
        out_specs=pl.BlockSpec((tm, D_MODEL), lambda i: (i, 0)),
        out_shape=jax.ShapeDtypeStruct((SEQ, D_MODEL), BF16),
        compiler_params=_params("parallel"),
        name="rmsnorm",
    )(x, g.reshape(1, D_MODEL))


def _glu_kernel(h_ref, wv_ref, wg_ref, wz_ref, u_ref, sz_ref):
    h = h_ref[...]
    val = jnp.dot(h, wv_ref[...], preferred_element_type=F32)
    gate = jnp.dot(h, wg_ref[...], preferred_element_type=F32)
    u_ref[...] = (val * _sigmoid(gate)).astype(u_ref.dtype)
    z = jnp.dot(h, wz_ref[...], preferred_element_type=F32)
    sz_ref[...] = _silu(z).astype(sz_ref.dtype)


def _glu_proj(h, w_in, tm=1024, tn=512):
    nb = CONV_WIDTH // tn
    out = jax.ShapeDtypeStruct((SEQ, CONV_WIDTH), BF16)
    return pl.pallas_call(
        _glu_kernel,
        grid=(SEQ // tm, nb),
        in_specs=[pl.BlockSpec((tm, D_MODEL), lambda i, j: (i, 0)),
                  pl.BlockSpec((D_MODEL, tn), lambda i, j: (0, j)),
                  pl.BlockSpec((D_MODEL, tn), lambda i, j: (0, nb + j)),
                  pl.BlockSpec((D_MODEL, tn), lambda i, j: (0, 2 * nb + j))],
        out_specs=[pl.BlockSpec((tm, tn), lambda i, j: (i, j)),
                   pl.BlockSpec((tm, tn), lambda i, j: (i, j))],
        out_shape=[out, out],
        compiler_params=_params("parallel", "arbitrary"),
        name="glu_proj",
    )(h, w_in, w_in, w_in)


def _conv_kernel(u_ref, halo_ref, sz_ref, w_ref, b_ref, g_ref, beta_ref, a_ref, buf_ref, y_ref, *, tm):
    i = pl.program_id(0)
    halo = halo_ref[...].astype(F32)
    buf_ref[0:CONV_HALO, :] = jnp.where(i == 0, 0.0, halo)
    buf_ref[CONV_HALO:CONV_HALO + tm, :] = u_ref[...].astype(F32)
    rb_rows = 64
    first = CONV_HALO - (CONV_KERNEL - 1)

    def col_body(cb, carry):
        cols = pl.ds(pl.multiple_of(cb * LANES, LANES), LANES)
        bias = b_ref[:, cols]
        for rb in range(tm // rb_rows):
            acc = jnp.broadcast_to(bias, (rb_rows, LANES))
            for k in range(CONV_KERNEL):
                acc = acc + w_ref[k:k + 1, cols] * buf_ref[pl.ds(rb * rb_rows + first + k, rb_rows), cols]
            y_ref[rb * rb_rows:(rb + 1) * rb_rows, cols] = acc
        return carry

    lax.fori_loop(0, CONV_WIDTH // LANES, col_body, 0)
    y = y_ref[...]
    mu = jnp.mean(y, axis=-1, keepdims=True)
    yc = y - mu
    var = jnp.mean(yc * yc, axis=-1, keepdims=True)
    yn = yc * lax.rsqrt(var + EPS) * g_ref[...] + beta_ref[...]
    a_ref[...] = (_silu(yn) * sz_ref[...].astype(F32)).astype(a_ref.dtype)


def _conv_gate(u, sz, dw_w, dw_b, ln_g, ln_b, tm=256):
    e = CONV_WIDTH
    w_pad = jnp.zeros((32, e), F32).at[:CONV_KERNEL].set(dw_w)
    hb = tm // CONV_HALO
    row = lambda v: v.reshape(1, e)
    return pl.pallas_call(
        functools.partial(_conv_kernel, tm=tm),
        grid=(SEQ // tm,),
        in_specs=[pl.BlockSpec((tm, e), lambda i: (i, 0)),
                  pl.BlockSpec((CONV_HALO, e), lambda i: (jnp.maximum(i * hb - 1, 0), 0)),
                  pl.BlockSpec((tm, e), lambda i: (i, 0)),
                  pl.BlockSpec((32, e), lambda i: (0, 0)),
                  pl.BlockSpec((1, e), lambda i: (0, 0)),
                  pl.BlockSpec((1, e), lambda i: (0, 0)),
                  pl.BlockSpec((1, e), lambda i: (0, 0))],
        out_specs=pl.BlockSpec((tm, e), lambda i: (i, 0)),
        out_shape=jax.ShapeDtypeStruct((SEQ, e), BF16),
        scratch_shapes=[pltpu.VMEM((CONV_HALO + tm, e), F32), pltpu.VMEM((tm, e), F32)],
        compiler_params=_params("parallel"),
        name="conv_gate",
    )(u, u, sz, w_pad, row(dw_b), row(ln_g), row(ln_b))


def _out_ple_kernel(a_ref, wout_ref, x_ref, p_ref, gw_ref, pw_ref, gn_ref, *out_refs, final):
    y = x_ref[...] + jnp.dot(a_ref[...], wout_ref[...], preferred_element_type=F32)
    gate = _sigmoid(jnp.dot(y.astype(BF16), gw_ref[...], preferred_element_type=F32))
    pp = jnp.dot(p_ref[...].astype(BF16), pw_ref[...], preferred_element_type=F32)
    xn = y + gate * pp
    ms = jnp.mean(xn * xn, axis=-1, keepdims=True)
    hn = xn * lax.rsqrt(ms + EPS) * gn_ref[...]
    if final:
        out_refs[0][...] = hn
    else:
        out_refs[0][...] = xn
        out_refs[1][...] = hn.astype(out_refs[1].dtype)


def _out_ple(a, w_out, x, p_i, gate_w, proj_w, g_next, final, tm=256):
    d = D_MODEL
    blk = pl.BlockSpec((tm, d), lambda i: (i, 0))
    if final:
        out_specs = [blk]
        out_shape = [jax.ShapeDtypeStruct((SEQ, d), F32)]
    else:
        out_specs = [blk, blk]
        out_shape = [jax.ShapeDtypeStruct((SEQ, d), F32), jax.ShapeDtypeStruct((SEQ, d), BF16)]
    return pl.pallas_call(
        functools.partial(_out_ple_kernel, final=final),
        grid=(SEQ // tm,),
        in_specs=[pl.BlockSpec((tm, VALUE_DIM), lambda i: (i, 0)),
                  _resident((VALUE_DIM, d)),
                  blk,
                  pl.BlockSpec((tm, PLE_DIM), lambda i: (i, 0)),
                  _resident((d, d)),
                  _resident((PLE_DIM, d)),
                  pl.BlockSpec((1, d), lambda i: (0, 0))],
        out_specs=out_specs,
        out_shape=out_shape,
        compiler_params=_params("parallel"),
        name="out_ple",
    )(a, w_out, x, p_i, gate_w, proj_w, g_next.reshape(1, d))


def _matmul_kernel(h_ref, w_ref, o_ref):
    o_ref[...] = jnp.dot(h_ref[...], w_ref[...], preferred_element_type=F32).astype(o_ref.dtype)


def _delta_proj(h, w, tm=1024, tn=1024):
    n = w.shape[1]
    return pl.pallas_call(
        _matmul_kernel,
        grid=(SEQ // tm, n // tn),
        in_specs=[pl.BlockSpec((tm, D_MODEL), lambda i, j: (i, 0)),
                  pl.BlockSpec((D_MODEL, tn), lambda i, j: (0, j))],
        out_specs=pl.BlockSpec((tm, tn), lambda i, j: (i, j)),
        out_shape=jax.ShapeDtypeStruct((SEQ, n), BF16),
        compiler_params=_params("parallel", "arbitrary"),
        name="delta_proj",
    )(h, w)


def _delta_prep_kernel(proj_ref, halo_ref, h_ref, wba_ref, cw_ref, alog_ref, dtb_ref,
                       q_ref, k_ref, v_ref, s1_ref, s2_ref, buf_ref, *, tm):
    i = pl.program_id(0)
    halo = halo_ref[...].astype(F32)
    buf_ref[0:SHORT_HALO, :] = jnp.where(i == 0, 0.0, halo)
    buf_ref[SHORT_HALO:SHORT_HALO + tm, :] = proj_ref[...].astype(F32)
    first = SHORT_HALO - (SHORT_CONV - 1)

    def conv_silu(cols):
        acc = cw_ref[0:1, cols] * buf_ref[pl.ds(first, tm), cols]
        for k in range(1, SHORT_CONV):
            acc = acc + cw_ref[k:k + 1, cols] * buf_ref[pl.ds(first + k, tm), cols]
        return _silu(acc)

    def l2n(y):
        return y * lax.rsqrt(jnp.sum(y * y, axis=-1, keepdims=True) + EPS)

    def q_body(c, carry):
        off = pl.multiple_of(c * LANES, LANES)
        y = conv_silu(pl.ds(off, LANES))
        q_ref[:, pl.ds(off, LANES)] = (l2n(y) * (HEAD_DIM ** -0.5)).astype(q_ref.dtype)
        return carry

    def k_body(c, carry):
        off = pl.multiple_of(c * LANES, LANES)
        y = conv_silu(pl.ds(pl.multiple_of(KEY_DIM + off, LANES), LANES))
        k_ref[:, pl.ds(off, LANES)] = l2n(y).astype(k_ref.dtype)
        return carry

    def v_body(c, carry):
        off = pl.multiple_of(c * LANES, LANES)
        y = conv_silu(pl.ds(pl.multiple_of(2 * KEY_DIM + off, LANES), LANES))
        v_ref[:, pl.ds(off, LANES)] = y.astype(v_ref.dtype)
        return carry

    lax.fori_loop(0, N_K_HEADS, q_body, 0)
    lax.fori_loop(0, N_K_HEADS, k_body, 0)
    lax.fori_loop(0, N_V_HEADS, v_body, 0)

    ba = jnp.dot(h_ref[...], wba_ref[...], preferred_element_type=F32)
    beta = _sigmoid(ba)
    xa = ba + dtb_ref[...]
    softplus = jnp.maximum(xa, 0.0) + jnp.log1p(jnp.exp(-jnp.abs(xa)))
    g = -jnp.exp(alog_ref[...]) * softplus
    pos = lax.broadcasted_iota(jnp.int32, (tm, LANES), 0) % CHUNK
    gc = g
    gs = g
    s = 1
    while s < CHUNK:
        gc = gc + jnp.where(pos >= s, pltpu.roll(gc, s, axis=0), 0.0)
        gs = gs + jnp.where(pos + s < CHUNK, pltpu.roll(gs, tm - s, axis=0), 0.0)
        s *= 2
    lane = lax.broadcasted_iota(jnp.int32, (tm, LANES), 1)
    s1_ref[...] = jnp.where(lane < N_V_HEADS, beta, gc)
    s2_ref[...] = gs - g


def _delta_prep(proj, h, w_ba, conv_w, a_log, dt_bias, tm=256):
    cw = jnp.zeros((8, QKV_DIM), F32).at[:SHORT_CONV].set(conv_w)
    pad = lambda v: jnp.zeros((1, LANES), F32).at[0, N_V_HEADS:2 * N_V_HEADS].set(v)
    hb = tm // SHORT_HALO
    small = jax.ShapeDtypeStruct((SEQ, LANES), F32)
    return pl.pallas_call(
        functools.partial(_delta_prep_kernel, tm=tm),
        grid=(SEQ // tm,),
        in_specs=[pl.BlockSpec((tm, QKV_DIM), lambda i: (i, 0)),
                  pl.BlockSpec((SHORT_HALO, QKV_DIM), lambda i: (jnp.maximum(i * hb - 1, 0), 0)),
                  pl.BlockSpec((tm, D_MODEL), lambda i: (i, 0)),
                  pl.BlockSpec((D_MODEL, LANES), lambda i: (0, 0)),
                  pl.BlockSpec((8, QKV_DIM), lambda i: (0, 0)),
                  pl.BlockSpec((1, LANES), lambda i: (0, 0)),
                  pl.BlockSpec((1, LANES), lambda i: (0, 0))],
        out_specs=[pl.BlockSpec((tm, KEY_DIM), lambda i: (i, 0)),
                   pl.BlockSpec((tm, KEY_DIM), lambda i: (i, 0)),
                   pl.BlockSpec((tm, VALUE_DIM), lambda i: (i, 0)),
                   pl.BlockSpec((tm, LANES), lambda i: (i, 0)),
                   pl.BlockSpec((tm, LANES), lambda i: (i, 0))],
        out_shape=[jax.ShapeDtypeStruct((SEQ, KEY_DIM), BF16),
                   jax.ShapeDtypeStruct((SEQ, KEY_DIM), BF16),
                   jax.ShapeDtypeStruct((SEQ, VALUE_DIM), BF16),
                   small, small],
        scratch_shapes=[pltpu.VMEM((SHORT_HALO + tm, QKV_DIM), F32)],
        compiler_params=_params("parallel"),
        name="delta_prep",
    )(proj, proj, h, w_ba, cw, pad(a_log), pad(dt_bias))


def _dot(a, b):
    return jnp.dot(a.astype(BF16), b.astype(BF16), preferred_element_type=F32)


def _dot_nt(a, b):
    return lax.dot_general(a.astype(BF16), b.astype(BF16), (((1,), (1,)), ((), ())),
                           preferred_element_type=F32)


def _dot_tn(a, b):
    return lax.dot_general(a.astype(BF16), b.astype(BF16), (((0,), (0,)), ((), ())),
                           preferred_element_type=F32)


def _delta_kernel(q_ref, k_ref, v_ref, z_ref, col_ref, row_ref, ng_ref, o_ref, s_ref, *, nc):
    @pl.when(pl.program_id(1) == 0)
    def _():
        s_ref[...] = jnp.zeros_like(s_ref)

    ii = lax.broadcasted_iota(jnp.int32, (CHUNK, CHUNK), 0)
    jj = lax.broadcasted_iota(jnp.int32, (CHUNK, CHUNK), 1)
    lower = ii >= jj
    strict = ii > jj
    eye = (ii == jj).astype(F32)
    for c in range(nc):
        rows = slice(c * CHUNK, (c + 1) * CHUNK)
        kc = k_ref[rows, :]
        qc = q_ref[rows, :]
        kk = _dot_nt(kc, kc)
        qk0 = _dot_nt(qc, kc)
        for e in range(2):
            hs = slice(e * HEAD_DIM, (e + 1) * HEAD_DIM)
            bcol = col_ref[rows, e:e + 1]
            gcol = col_ref[rows, 2 + e:3 + e]
            grcol = col_ref[rows, 4 + e:5 + e]
            brow = row_ref[c, e:e + 1, 0:CHUNK]
            grow = row_ref[c, 2 + e:3 + e, 0:CHUNK]
            gl = row_ref[c, 2 + e:3 + e, :] + row_ref[c, 4 + e:5 + e, :]
            dec = jnp.where(lower, jnp.exp(jnp.where(lower, gcol - grow, 0.0)), 0.0)
            a = jnp.where(strict, -(kk * bcol * dec), 0.0)
            p = eye + a
            ak = _dot(a, a)
            for _ in range(4):
                r = _dot(jnp.concatenate([p, ak], axis=0), ak)
                p = p + r[:CHUNK]
                ak = r[CHUNK:]
            p = p + _dot(p, ak)
            tb = p * brow
            u = _dot(tb, v_ref[rows, hs])
            w = _dot(tb * jnp.exp(grow), kc)
            qk = qk0 * dec
            kd = kc.astype(F32) * jnp.exp(grcol)
            sm = s_ref[e]
            sb = sm.astype(BF16)
            vnew = u - _dot(w, sb)
            o = _dot(qc, sb) * jnp.exp(gcol) + _dot(qk, vnew)
            s_ref[e] = sm * jnp.exp(gl) + _dot_tn(kd, vnew)
            ms = jnp.mean(o * o, axis=-1, keepdims=True)
            on = o * lax.rsqrt(ms + EPS) * ng_ref[...]
            o_ref[rows, hs] = (on * _silu(z_ref[rows, hs].astype(F32))).astype(o_ref.dtype)


def _delta_rule(qn, kn, vv, proj, col, row, norm_g, nc=4):
    tb = nc * CHUNK
    pair = 2 * HEAD_DIM
    z_off = QKV_DIM // pair
    return pl.pallas_call(
        functools.partial(_delta_kernel, nc=nc),
        grid=(N_K_HEADS, SEQ // tb),
        in_specs=[pl.BlockSpec((tb, HEAD_DIM), lambda h, t: (t, h)),
                  pl.BlockSpec((tb, HEAD_DIM), lambda h, t: (t, h)),
                  pl.BlockSpec((tb, pair), lambda h, t: (t, h)),
                  pl.BlockSpec((tb, pair), lambda h, t: (t, z_off + h)),
                  pl.BlockSpec((None, tb, 8), lambda h, t: (h, t, 0)),
                  pl.BlockSpec((None, nc, 8, LANES), lambda h, t: (h, t, 0, 0)),
                  pl.BlockSpec((1, HEAD_DIM), lambda h, t: (0, 0))],
        out_specs=pl.BlockSpec((tb, pair), lambda h, t: (t, h)),
        out_shape=jax.ShapeDtypeStruct((SEQ, VALUE_DIM), BF16),
        scratch_shapes=[pltpu.VMEM((2, HEAD_DIM, HEAD_DIM), F32)],
        compiler_params=_params("parallel", "arbitrary"),
        name="delta_rule",
    )(qn, kn, vv, proj, col, row, norm_g.reshape(1, HEAD_DIM))


def _gate_layouts(s1, s2):
    pairs = lambda v: v.reshape(SEQ, N_K_HEADS, 2)
    beta = pairs(s1[:, :N_V_HEADS])
    gc = pairs(s1[:, N_V_HEADS:2 * N_V_HEADS])
    gr = pairs(s2[:, N_V_HEADS:2 * N_V_HEADS])
    col = jnp.concatenate([beta, gc, gr, jnp.zeros_like(beta)], axis=-1).transpose(1, 0, 2)
    row = col.reshape(N_K_HEADS, N_CHUNKS, CHUNK, 8).transpose(0, 1, 3, 2)
    row = jnp.concatenate([row, row], axis=-1)
    return col, row


def kernel(x, p, conv_norm_g, conv_w_in, conv_dw_w, conv_dw_b, conv_ln_g, conv_ln_b, conv_w_out,
           delta_norm_g, delta_w_in, delta_conv_w, delta_a_log, delta_dt_bias, delta_out_norm_g,
           delta_w_out, ple_gate_w, ple_proj_w, final_norm_g):
    x = x.reshape(SEQ, D_MODEL)
    norm_gains = [conv_norm_g[0], delta_norm_g[0], conv_norm_g[1], delta_norm_g[1], final_norm_g]
    h = _rmsnorm(x, norm_gains[0])
    out = None
    for i in range(DEPTH):
        j = i // 2
        if i % 2 == 0:
            u, sz = _glu_proj(h, conv_w_in[j].astype(BF16))
            a = _conv_gate(u, sz, conv_dw_w[j], conv_dw_b[j], conv_ln_g[j], conv_ln_b[j])
            w_out = conv_w_out[j]
        else:
            w_in = delta_w_in[j]
            n_main = QKV_DIM + VALUE_DIM
            proj = _delta_proj(h, w_in[:, :n_main].astype(BF16))
            w_ba = jnp.zeros((D_MODEL, LANES), BF16).at[:, :2 * N_V_HEADS].set(w_in[:, n_main:].astype(BF16))
            qn, kn, vv, s1, s2 = _delta_prep(proj, h, w_ba, delta_conv_w[j], delta_a_log[j], delta_dt_bias[j])
            col, row = _gate_layouts(s1, s2)
            a = _delta_rule(qn, kn, vv, proj, col, row, delta_out_norm_g[j])
            w_out = delta_w_out[j]
        final = i == DEPTH - 1
        res = _out_ple(a, w_out.astype(BF16), x, p[i, 0], ple_gate_w[i].astype(BF16),
                       ple_proj_w[i].astype(BF16), norm_gains[i + 1], final)
        if final:
            out = res[0]
        else:
            x, h = res
    return out.reshape(1, SEQ, D_MODEL)
```

```python
import functools

import jax
import jax.numpy as jnp
from jax import lax
from jax.experimental import pallas as pl
from jax.experimental.pallas import tpu as pltpu

F32 = jnp.float32
BF16 = jnp.bfloat16

SEQ = 8192
D_MODEL = 2048
DEPTH = 4
PLE_DIM = 256
EPS = 1e-6
CONV_WIDTH = 2 * D_MODEL
CONV_KERNEL = 31
HEAD_DIM = 128
N_K_HEADS = 16
N_V_HEADS = 32
KEY_DIM = N_K_HEADS * HEAD_DIM
VALUE_DIM = N_V_HEADS * HEAD_DIM
QKV_DIM = 2 * KEY_DIM + VALUE_DIM
SHORT_CONV = 4
CHUNK = 128
N_CHUNKS = SEQ // CHUNK

LANES = 128
VMEM_LIMIT = 56 * 1024 * 1024

CONV_HALO = 32
SHORT_HALO = 16


def _sigmoid(x):
    return 1.0 / (1.0 + jnp.exp(-x))


def _silu(x):
    return x * _sigmoid(x)


def _params(*sem):
    return pltpu.CompilerParams(dimension_semantics=sem, vmem_limit_bytes=VMEM_LIMIT)


def _resident(shape):
    nd = len(shape)
    return pl.BlockSpec(shape, lambda *_: (0,) * nd, pipeline_mode=pl.Buffered(1))


def _rmsnorm_kernel(x_ref, g_ref, h_ref):
    x = x_ref[...]
    ms = jnp.mean(x * x, axis=-1, keepdims=True)
    h_ref[...] = (x * lax.rsqrt(ms + EPS) * g_ref[...]).astype(h_ref.dtype)


def _rmsnorm(x, g, tm=512):
    return pl.pallas_call(
        _rmsnorm_kernel,
        grid=(SEQ // tm,),
        in_specs=[pl.BlockSpec((tm, D_MODEL), lambda i: (i, 0)),
                  pl.BlockSpec((1, D_MODEL), lambda i: (0, 0))],
        out_specs=pl.BlockSpec((tm, D_MODEL), lambda i: (i, 0)),
        out_shape=jax.ShapeDtypeStruct((SEQ, D_MODEL), BF16),
        compiler_params=_params("parallel"),
        name="rmsnorm",
    )(x, g.reshape(1, D_MODEL))


def _glu_kernel(h_ref, wv_ref, wg_ref, wz_ref, u_ref, sz_ref):
    h = h_ref[...]
    val = jnp.dot(h, wv_ref[...], preferred_element_type=F32)
    gate = jnp.dot(h, wg_ref[...], preferred_element_type=F32)
    u_ref[...] = (val * _sigmoid(gate)).astype(u_ref.dtype)
    z = jnp.dot(h, wz_ref[...], preferred_element_type=F32)
    sz_ref[...] = _silu(z).astype(sz_ref.dtype)


def _glu_proj(h, w_in, tm=1024, tn=512):
    nb = CONV_WIDTH // tn
    out = jax.ShapeDtypeStruct((SEQ, CONV_WIDTH), BF16)
    return pl.pallas_call(
        _glu_kernel,
        grid=(SEQ // tm, nb),
        in_specs=[pl.BlockSpec((tm, D_MODEL), lambda i, j: (i, 0)),
                  pl.BlockSpec((D_MODEL, tn), lambda i, j: (0, j)),
                  pl.BlockSpec((D_MODEL, tn), lambda i, j: (0, nb + j)),
                  pl.BlockSpec((D_MODEL, tn), lambda i, j: (0, 2 * nb + j))],
        out_specs=[pl.BlockSpec((tm, tn), lambda i, j: (i, j)),
                   pl.BlockSpec((tm, tn), lambda i, j: (i, j))],
        out_shape=[out, out],
        compiler_params=_params("parallel", "arbitrary"),
        name="glu_proj",
    )(h, w_in, w_in, w_in)


def _conv_kernel(u_ref, halo_ref, sz_ref, w_ref, b_ref, g_ref, beta_ref, a_ref, buf_ref, y_ref, *, tm):
    i = pl.program_id(0)
    halo = halo_ref[...].astype(F32)
    buf_ref[0:CONV_HALO, :] = jnp.where(i == 0, 0.0, halo)
    buf_ref[CONV_HALO:CONV_HALO + tm, :] = u_ref[...].astype(F32)
    rb_rows = 64
    first = CONV_HALO - (CONV_KERNEL - 1)

    def col_body(cb, carry):
        cols = pl.ds(pl.multiple_of(cb * LANES, LANES), LANES)
        bias = b_ref[:, cols]
        for rb in range(tm // rb_rows):
            acc = jnp.broadcast_to(bias, (rb_rows, LANES))
            for k in range(CONV_KERNEL):
                acc = acc + w_ref[k:k + 1, cols] * buf_ref[pl.ds(rb * rb_rows + first + k, rb_rows), cols]
            y_ref[rb * rb_rows:(rb + 1) * rb_rows, cols] = acc
        return carry

    lax.fori_loop(0, CONV_WIDTH // LANES, col_body, 0)
    y = y_ref[...]
    mu = jnp.mean(y, axis=-1, keepdims=True)
    yc = y - mu
    var = jnp.mean(yc * yc, axis=-1, keepdims=True)
    yn = yc * lax.rsqrt(var + EPS) * g_ref[...] + beta_ref[...]
    a_ref[...] = (_silu(yn) * sz_ref[...].astype(F32)).astype(a_ref.dtype)


def _conv_gate(u, sz, dw_w, dw_b, ln_g, ln_b, tm=256):
    e = CONV_WIDTH
    w_pad = jnp.zeros((32, e), F32).at[:CONV_KERNEL].set(dw_w)
    hb = tm // CONV_HALO
    row = lambda v: v.reshape(1, e)
    return pl.pallas_call(
        functools.partial(_conv_kernel, tm=tm),
        grid=(SEQ // tm,),
        in_specs=[pl.BlockSpec((tm, e), lambda i: (i, 0)),
                  pl.BlockSpec((CONV_HALO, e), lambda i: (jnp.maximum(i * hb - 1, 0), 0)),
                  pl.BlockSpec((tm, e), lambda i: (i, 0)),
                  pl.BlockSpec((32, e), lambda i: (0, 0)),
                  pl.BlockSpec((1, e), lambda i: (0, 0)),
                  pl.BlockSpec((1, e), lambda i: (0, 0)),
                  pl.BlockSpec((1, e), lambda i: (0, 0))],
        out_specs=pl.BlockSpec((tm, e), lambda i: (i, 0)),
        out_shape=jax.ShapeDtypeStruct((SEQ, e), BF16),
        scratch_shapes=[pltpu.VMEM((CONV_HALO + tm, e), F32), pltpu.VMEM((tm, e), F32)],
        compiler_params=_params("parallel"),
        name="conv_gate",
    )(u, u, sz, w_pad, row(dw_b), row(ln_g), row(ln_b))


def _out_ple_kernel(a_ref, wout_ref, x_ref, p_ref, gw_ref, pw_ref, gn_ref, *out_refs, final):
    y = x_ref[...] + jnp.dot(a_ref[...], wout_ref[...], preferred_element_type=F32)
    gate = _sigmoid(jnp.dot(y.astype(BF16), gw_ref[...], preferred_element_type=F32))
    pp = jnp.dot(p_ref[...].astype(BF16), pw_ref[...], preferred_element_type=F32)
    xn = y + gate * pp
    ms = jnp.mean(xn * xn, axis=-1, keepdims=True)
    hn = xn * lax.rsqrt(ms + EPS) * gn_ref[...]
    if final:
        out_refs[0][...] = hn
    else:
        out_refs[0][...] = xn
        out_refs[1][...] = hn.astype(out_refs[1].dtype)


def _out_ple(a, w_out, x, p_i, gate_w, proj_w, g_next, final, tm=256):
    d = D_MODEL
    blk = pl.BlockSpec((tm, d), lambda i: (i, 0))
    if final:
        out_specs = [blk]
        out_shape = [jax.ShapeDtypeStruct((SEQ, d), F32)]
    else:
        out_specs = [blk, blk]
        out_shape = [jax.ShapeDtypeStruct((SEQ, d), F32), jax.ShapeDtypeStruct((SEQ, d), BF16)]
    return pl.pallas_call(
        functools.partial(_out_ple_kernel, final=final),
        grid=(SEQ // tm,),
        in_specs=[pl.BlockSpec((tm, VALUE_DIM), lambda i: (i, 0)),
                  _resident((VALUE_DIM, d)),
                  blk,
                  pl.BlockSpec((tm, PLE_DIM), lambda i: (i, 0)),
                  _resident((d, d)),
                  _resident((PLE_DIM, d)),
                  pl.BlockSpec((1, d), lambda i: (0, 0))],
        out_specs=out_specs,
        out_shape=out_shape,
        compiler_params=_params("parallel"),
        name="out_ple",
    )(a, w_out, x, p_i, gate_w, proj_w, g_next.reshape(1, d))


def _matmul_kernel(h_ref, w_ref, o_ref):
    o_ref[...] = jnp.dot(h_ref[...], w_ref[...], preferred_element_type=F32).astype(o_ref.dtype)


def _delta_proj(h, w, tm=1024, tn=1024):
    n = w.shape[1]
    return pl.pallas_call(
        _matmul_kernel,
        grid=(SEQ // tm, n // tn),
        in_specs=[pl.BlockSpec((tm, D_MODEL), lambda i, j: (i, 0)),
                  pl.BlockSpec((D_MODEL, tn), lambda i, j: (0, j))],
        out_specs=pl.BlockSpec((tm, tn), lambda i, j: (i, j)),
        out_shape=jax.ShapeDtypeStruct((SEQ, n), BF16),
        compiler_params=_params("parallel", "arbitrary"),
        name="delta_proj",
    )(h, w)


def _delta_prep_kernel(proj_ref, halo_ref, h_ref, wba_ref, cw_ref, alog_ref, dtb_ref,
                       q_ref, k_ref, kt_ref, v_ref, s1_ref, s2_ref, buf_ref, *, tm):
    i = pl.program_id(0)
    halo = halo_ref[...].astype(F32)
    buf_ref[0:SHORT_HALO, :] = jnp.where(i == 0, 0.0, halo)
    buf_ref[SHORT_HALO:SHORT_HALO + tm, :] = proj_ref[...].astype(F32)
    first = SHORT_HALO - (SHORT_CONV - 1)

    def conv_silu(cols):
        acc = cw_ref[0:1, cols] * buf_ref[pl.ds(first, tm), cols]
        for k in range(1, SHORT_CONV):
            acc = acc + cw_ref[k:k + 1, cols] * buf_ref[pl.ds(first + k, tm), cols]
        return _silu(acc)

    def l2n(y):
        return y * lax.rsqrt(jnp.sum(y * y, axis=-1, keepdims=True) + EPS)

    def q_body(c, carry):
        off = pl.multiple_of(c * LANES, LANES)
        y = conv_silu(pl.ds(off, LANES))
        q_ref[:, pl.ds(off, LANES)] = (l2n(y) * (HEAD_DIM ** -0.5)).astype(q_ref.dtype)
        return carry

    def k_body(c, carry):
        off = pl.multiple_of(c * LANES, LANES)
        y = conv_silu(pl.ds(pl.multiple_of(KEY_DIM + off, LANES), LANES))
        kn = l2n(y)
        k_ref[:, pl.ds(off, LANES)] = kn.astype(k_ref.dtype)
        kt_ref[pl.ds(off, LANES), :] = kn.T.astype(kt_ref.dtype)
        return carry

    def v_body(c, carry):
        off = pl.multiple_of(c * LANES, LANES)
        y = conv_silu(pl.ds(pl.multiple_of(2 * KEY_DIM + off, LANES), LANES))
        v_ref[:, pl.ds(off, LANES)] = y.astype(v_ref.dtype)
        return carry

    lax.fori_loop(0, N_K_HEADS, q_body, 0)
    lax.fori_loop(0, N_K_HEADS, k_body, 0)
    lax.fori_loop(0, N_V_HEADS, v_body, 0)

    ba = jnp.dot(h_ref[...], wba_ref[...], preferred_element_type=F32)
    beta = _sigmoid(ba)
    xa = ba + dtb_ref[...]
    softplus = jnp.maximum(xa, 0.0) + jnp.log1p(jnp.exp(-jnp.abs(xa)))
    g = -jnp.exp(alog_ref[...]) * softplus
    pos = lax.broadcasted_iota(jnp.int32, (tm, LANES), 0) % CHUNK
    gc = g
    gs = g
    s = 1
    while s < CHUNK:
        gc = gc + jnp.where(pos >= s, pltpu.roll(gc, s, axis=0), 0.0)
        gs = gs + jnp.where(pos + s < CHUNK, pltpu.roll(gs, tm - s, axis=0), 0.0)
        s *= 2
    lane = lax.broadcasted_iota(jnp.int32, (tm, LANES), 1)
    s1_ref[...] = jnp.where(lane < N_V_HEADS, beta, gc)
    s2_ref[...] = gs - g


def _delta_prep(proj, h, w_ba, conv_w, a_log, dt_bias, tm=256):
    cw = jnp.zeros((8, QKV_DIM), F32).at[:SHORT_CONV].set(conv_w)
    pad = lambda v: jnp.zeros((1, LANES), F32).at[0, N_V_HEADS:2 * N_V_HEADS].set(v)
    hb = tm // SHORT_HALO
    small = jax.ShapeDtypeStruct((SEQ, LANES), F32)
    return pl.pallas_call(
        functools.partial(_delta_prep_kernel, tm=tm),
        grid=(SEQ // tm,),
        in_specs=[pl.BlockSpec((tm, QKV_DIM), lambda i: (i, 0)),
                  pl.BlockSpec((SHORT_HALO, QKV_DIM), lambda i: (jnp.maximum(i * hb - 1, 0), 0)),
                  pl.BlockSpec((tm, D_MODEL), lambda i: (i, 0)),
                  pl.BlockSpec((D_MODEL, LANES), lambda i: (0, 0)),
                  pl.BlockSpec((8, QKV_DIM), lambda i: (0, 0)),
                  pl.BlockSpec((1, LANES), lambda i: (0, 0)),
                  pl.BlockSpec((1, LANES), lambda i: (0, 0))],
        out_specs=[pl.BlockSpec((tm, KEY_DIM), lambda i: (i, 0)),
                   pl.BlockSpec((tm, KEY_DIM), lambda i: (i, 0)),
                   pl.BlockSpec((KEY_DIM, tm), lambda i: (0, i)),
                   pl.BlockSpec((tm, VALUE_DIM), lambda i: (i, 0)),
                   pl.BlockSpec((tm, LANES), lambda i: (i, 0)),
                   pl.BlockSpec((tm, LANES), lambda i: (i, 0))],
        out_shape=[jax.ShapeDtypeStruct((SEQ, KEY_DIM), BF16),
                   jax.ShapeDtypeStruct((SEQ, KEY_DIM), BF16),
                   jax.ShapeDtypeStruct((KEY_DIM, SEQ), BF16),
                   jax.ShapeDtypeStruct((SEQ, VALUE_DIM), BF16),
                   small, small],
        scratch_shapes=[pltpu.VMEM((SHORT_HALO + tm, QKV_DIM), F32)],
        compiler_params=_params("parallel"),
        name="delta_prep",
    )(proj, proj, h, w_ba, cw, pad(a_log), pad(dt_bias))


def _dot(a, b):
    return jnp.dot(a.astype(BF16), b.astype(BF16), preferred_element_type=F32)


def _delta_kernel(q_ref, k_ref, kt_ref, v_ref, z_ref, col_ref, row_ref, ng_ref, o_ref, s_ref, *, nc, npairs):
    @pl.when(pl.program_id(1) == 0)
    def _():
        s_ref[...] = jnp.zeros_like(s_ref)

    ii = lax.broadcasted_iota(jnp.int32, (CHUNK, CHUNK), 0)
    jj = lax.broadcasted_iota(jnp.int32, (CHUNK, CHUNK), 1)
    lower = ii >= jj
    strict = ii > jj
    eye = (ii == jj).astype(F32)
    blk_xor = ii ^ jj
    rows = lambda c: slice(c * CHUNK, (c + 1) * CHUNK)
    kcols = lambda pr: slice(pr * HEAD_DIM, (pr + 1) * HEAD_DIM)
    vcols = lambda pr, e: slice((2 * pr + e) * HEAD_DIM, (2 * pr + e + 1) * HEAD_DIM)
    units = [(c, pr, e) for c in range(nc) for pr in range(npairs) for e in range(2)]

    kk, qk0 = {}, {}
    for c in range(nc):
        for pr in range(npairs):
            ktc = kt_ref[kcols(pr), rows(c)]
            kk[c, pr] = _dot(k_ref[rows(c), kcols(pr)], ktc)
            qk0[c, pr] = _dot(q_ref[rows(c), kcols(pr)], ktc)

    dec, a, t = {}, {}, {}
    for un in units:
        c, pr, e = un
        bcol = col_ref[pr, rows(c), e:e + 1]
        gcol = col_ref[pr, rows(c), 2 + e:3 + e]
        grow = row_ref[pr, c, 2 + e:3 + e, :]
        dec[un] = jnp.where(lower, jnp.exp(jnp.where(lower, gcol - grow, 0.0)), 0.0)
        a[un] = jnp.where(strict, -(kk[c, pr] * bcol * dec[un]), 0.0)
        t[un] = eye + jnp.where(blk_xor < 2, a[un], 0.0)
    b = 2
    while b < CHUNK:
        coupling = (blk_xor >= b) & (blk_xor < 2 * b)
        x = {un: _dot(jnp.where(coupling, a[un], 0.0), t[un]) for un in units}
        for un in units:
            t[un] = t[un] + _dot(t[un], x[un])
        b *= 2
    uu, ww, qk = {}, {}, {}
    for un in units:
        c, pr, e = un
        tb = t[un] * row_ref[pr, c, e:e + 1, :]
        uu[un] = _dot(tb, v_ref[rows(c), vcols(pr, e)])
        ww[un] = _dot(tb * jnp.exp(row_ref[pr, c, 2 + e:3 + e, :]), k_ref[rows(c), kcols(pr)]).astype(BF16)
        qk[un] = (qk0[c, pr] * dec[un]).astype(BF16)

    heads = [(pr, e) for pr in range(npairs) for e in range(2)]
    for c in range(nc):
        sm, r = {}, {}
        for pr, e in heads:
            sm[pr, e] = s_ref[2 * pr + e]
            r[pr, e] = _dot(jnp.concatenate([ww[c, pr, e], q_ref[rows(c), kcols(pr)]], axis=0), sm[pr, e])
        for pr, e in heads:
            un = (c, pr, e)
            grrow = row_ref[pr, c, 4 + e:5 + e, :]
            gl = row_ref[pr, c, 2 + e:3 + e, :] + grrow
            vnew = (uu[un] - r[pr, e][:CHUNK]).astype(BF16)
            o = r[pr, e][CHUNK:] * jnp.exp(col_ref[pr, rows(c), 2 + e:3 + e]) + _dot(qk[un], vnew)
            kdt = kt_ref[kcols(pr), rows(c)].astype(F32) * jnp.exp(grrow)
            s_ref[2 * pr + e] = sm[pr, e] * jnp.exp(gl) + _dot(kdt, vnew)
            ms = jnp.mean(o * o, axis=-1, keepdims=True)
            on = o * lax.rsqrt(ms + EPS) * ng_ref[...]
            zc = z_ref[rows(c), vcols(pr, e)].astype(F32)
            o_ref[rows(c), vcols(pr, e)] = (on * _silu(zc)).astype(o_ref.dtype)


def _delta_rule(qn, kn, kt, vv, proj, col, row, norm_g, nc=2, npairs=4):
    tb = nc * CHUNK
    kw = npairs * HEAD_DIM
    vw = 2 * kw
    z_off = QKV_DIM // vw
    return pl.pallas_call(
        functools.partial(_delta_kernel, nc=nc, npairs=npairs),
        grid=(N_K_HEADS // npairs, SEQ // tb),
        in_specs=[pl.BlockSpec((tb, kw), lambda g, t: (t, g)),
                  pl.BlockSpec((tb, kw), lambda g, t: (t, g)),
                  pl.BlockSpec((kw, tb), lambda g, t: (g, t)),
                  pl.BlockSpec((tb, vw), lambda g, t: (t, g)),
                  pl.BlockSpec((tb, vw), lambda g, t: (t, z_off + g)),
                  pl.BlockSpec((npairs, tb, 8), lambda g, t: (g, t, 0)),
                  pl.BlockSpec((npairs, nc, 8, CHUNK), lambda g, t: (g, t, 0, 0)),
                  pl.BlockSpec((1, HEAD_DIM), lambda g, t: (0, 0))],
        out_specs=pl.BlockSpec((tb, vw), lambda g, t: (t, g)),
        out_shape=jax.ShapeDtypeStruct((SEQ, VALUE_DIM), BF16),
        scratch_shapes=[pltpu.VMEM((2 * npairs, HEAD_DIM, HEAD_DIM), F32)],
        compiler_params=_params("parallel", "arbitrary"),
        name="delta_rule",
    )(qn, kn, kt, vv, proj, col, row, norm_g.reshape(1, HEAD_DIM))


def _gate_layouts(s1, s2):
    pairs = lambda v: v.reshape(SEQ, N_K_HEADS, 2)
    beta = pairs(s1[:, :N_V_HEADS])
    gc = pairs(s1[:, N_V_HEADS:2 * N_V_HEADS])
    gr = pairs(s2[:, N_V_HEADS:2 * N_V_HEADS])
    col = jnp.concatenate([beta, gc, gr, jnp.zeros_like(beta)], axis=-1).transpose(1, 0, 2)
    row = col.reshape(N_K_HEADS, N_CHUNKS, CHUNK, 8).transpose(0, 1, 3, 2)
    return col, row


def kernel(x, p, conv_norm_g, conv_w_in, conv_dw_w, conv_dw_b, conv_ln_g, conv_ln_b, conv_w_out,
           delta_norm_g, delta_w_in, delta_conv_w, delta_a_log, delta_dt_bias, delta_out_norm_g,
           delta_w_out, ple_gate_w, ple_proj_w, final_norm_g):
    x = x.reshape(SEQ, D_MODEL)
    norm_gains = [conv_norm_g[0], delta_norm_g[0], conv_norm_g[1], delta_norm_g[1], final_norm_g]
    h = _rmsnorm(x, norm_gains[0])
    out = None
    for i in range(DEPTH):
        j = i // 2
        if i % 2 == 0:
            u, sz = _glu_proj(h, conv_w_in[j].astype(BF16))
            a = _conv_gate(u, sz, conv_dw_w[j], conv_dw_b[j], conv_ln_g[j], conv_ln_b[j])
            w_out = conv_w_out[j]
        else:
            w_in = delta_w_in[j]
            n_main = QKV_DIM + VALUE_DIM
            proj = _delta_proj(h, w_in[:, :n_main].astype(BF16))
            w_ba = jnp.zeros((D_MODEL, LANES), BF16).at[:, :2 * N_V_HEADS].set(w_in[:, n_main:].astype(BF16))
            qn, kn, kt, vv, s1, s2 = _delta_prep(proj, h, w_ba, delta_conv_w[j], delta_a_log[j], delta_dt_bias[j])
            col, row = _gate_layouts(s1, s2)
            a = _delta_rule(qn, kn, kt, vv, proj, col, row, delta_out_norm_g[j])
            w_out = delta_w_out[j]
        final = i == DEPTH - 1
        res = _out_ple(a, w_out.astype(BF16), x, p[i, 0], ple_gate_w[i].astype(BF16),
                       ple_proj_w[i].astype(BF16), norm_gains[i + 1], final)
        if final:
            out = res[0]
        else:
            x, h = res
    return out.reshape(1, SEQ, D_MODEL)
```

```python
import functools

import jax
import jax.numpy as jnp
from jax import lax
from jax.experimental import pallas as pl
from jax.experimental.pallas import tpu as pltpu

F32 = jnp.float32
BF16 = jnp.bfloat16

SEQ = 8192
D_MODEL = 2048
DEPTH = 4
PLE_DIM = 256
EPS = 1e-6
CONV_WIDTH = 2 * D_MODEL
CONV_KERNEL = 31
HEAD_DIM = 128
N_K_HEADS = 16
N_V_HEADS = 32
KEY_DIM = N_K_HEADS * HEAD_DIM
VALUE_DIM = N_V_HEADS * HEAD_DIM
QKV_DIM = 2 * KEY_DIM + VALUE_DIM
SHORT_CONV = 4
CHUNK = 128
N_CHUNKS = SEQ // CHUNK

LANES = 128
SUBLANES = 8
MXU_WIDTH = 256
VMEM_LIMIT = 56 * 1024 * 1024

CONV_HALO = 32


def _sigmoid(x):
    return 1.0 / (1.0 + jnp.exp(-x))


def _silu(x):
    return x * _sigmoid(x)


def _params(*sem):
    return pltpu.CompilerParams(dimension_semantics=sem, vmem_limit_bytes=VMEM_LIMIT)


def _resident(shape):
    nd = len(shape)
    return pl.BlockSpec(shape, lambda *_: (0,) * nd, pipeline_mode=pl.Buffered(1))


def _rmsnorm_kernel(x_ref, g_ref, h_ref):
    x = x_ref[...]
    ms = jnp.mean(x * x, axis=-1, keepdims=True)
    h_ref[...] = (x * lax.rsqrt(ms + EPS) * g_ref[...]).astype(h_ref.dtype)


def _rmsnorm(x, g, tm=512):
    return pl.pallas_call(
        _rmsnorm_kernel,
        grid=(SEQ // tm,),
        in_specs=[pl.BlockSpec((tm, D_MODEL), lambda i: (i, 0)),
                  pl.BlockSpec((1, D_MODEL), lambda i: (0, 0))],
        out_specs=pl.BlockSpec((tm, D_MODEL), lambda i: (i, 0)),
        out_shape=jax.ShapeDtypeStruct((SEQ, D_MODEL), BF16),
        compiler_params=_params("parallel"),
        name="rmsnorm",
    )(x, g.reshape(1, D_MODEL))


def _glu_kernel(h_ref, wv_ref, wg_ref, wz_ref, u_ref, sz_ref, wb_ref):
    @pl.when(pl.program_id(1) == 0)
    def _():
        wb_ref[0] = wv_ref[...].astype(BF16)
        wb_ref[1] = wg_ref[...].astype(BF16)
        wb_ref[2] = wz_ref[...].astype(BF16)

    h = h_ref[...]
    val = jnp.dot(h, wb_ref[0], preferred_element_type=F32)
    gate = jnp.dot(h, wb_ref[1], preferred_element_type=F32)
    u_ref[...] = (val * _sigmoid(gate)).astype(u_ref.dtype)
    z = jnp.dot(h, wb_ref[2], preferred_element_type=F32)
    sz_ref[...] = _silu(z).astype(sz_ref.dtype)


def _glu_proj(h, w_in_all, layer, tm=1024, tn=512):
    nb = CONV_WIDTH // tn
    out = jax.ShapeDtypeStruct((SEQ, CONV_WIDTH), BF16)
    wspec = lambda part: pl.BlockSpec((None, D_MODEL, tn), lambda j, i: (layer, 0, part * nb + j))
    return pl.pallas_call(
        _glu_kernel,
        grid=(nb, SEQ // tm),
        in_specs=[pl.BlockSpec((tm, D_MODEL), lambda j, i: (i, 0)), wspec(0), wspec(1), wspec(2)],
        out_specs=[pl.BlockSpec((tm, tn), lambda j, i: (i, j)),
                   pl.BlockSpec((tm, tn), lambda j, i: (i, j))],
        out_shape=[out, out],
        scratch_shapes=[pltpu.VMEM((3, D_MODEL, tn), BF16)],
        compiler_params=_params("parallel", "arbitrary"),
        name="glu_proj",
    )(h, w_in_all, w_in_all, w_in_all)


def _conv_kernel(u_ref, halo_ref, sz_ref, w_ref, b_ref, g_ref, beta_ref, a_ref, sh_ref, y_ref, s1_ref, *, tm):
    i = pl.program_id(0)
    ext = CONV_HALO + tm
    first = CONV_HALO - (CONV_KERNEL - 1)
    rb_rows = 64
    n_rb = tm // rb_rows
    n_cb = CONV_WIDTH // LANES
    inv_e = 1.0 / CONV_WIDTH
    s1_ref[...] = jnp.zeros_like(s1_ref)

    def lanes(cb):
        return pl.ds(pl.multiple_of(cb * LANES, LANES), LANES)

    def conv_body(cb, carry):
        cols = lanes(cb)
        halo = jnp.where(i == 0, 0.0, halo_ref[:, cols].astype(F32))
        x = jnp.concatenate([halo, u_ref[:, cols].astype(F32)], axis=0)
        sh_ref[0] = x
        for r in range(1, SUBLANES):
            sh_ref[r] = pltpu.roll(x, ext - r, axis=0)
        bias = b_ref[:, cols]
        for rb in range(n_rb):
            acc = jnp.broadcast_to(bias, (rb_rows, LANES))
            for k in range(CONV_KERNEL):
                r = (first + k) % SUBLANES
                base = first + k - r
                acc = acc + w_ref[k:k + 1, cols] * sh_ref[r, pl.ds(rb * rb_rows + base, rb_rows), :]
            rs = slice(rb * rb_rows, (rb + 1) * rb_rows)
            y_ref[rs, cols] = acc
            s1_ref[rs, :] += acc
        return carry

    lax.fori_loop(0, n_cb, conv_body, 0)

    for rb in range(n_rb):
        rs = slice(rb * rb_rows, (rb + 1) * rb_rows)
        mu = jnp.broadcast_to(jnp.sum(s1_ref[rs, :], axis=-1, keepdims=True) * inv_e, (rb_rows, LANES))

        def var_body(cb, acc2):
            d = y_ref[rs, lanes(cb)] - mu
            return acc2 + d * d

        acc2 = lax.fori_loop(0, n_cb, var_body, jnp.zeros((rb_rows, LANES), F32), unroll=4)
        var = jnp.sum(acc2, axis=-1, keepdims=True) * inv_e
        rstd = jnp.broadcast_to(lax.rsqrt(var + EPS), (rb_rows, LANES))

        def out_body(cb, carry):
            cols = lanes(cb)
            yn = (y_ref[rs, cols] - mu) * rstd * g_ref[:, cols] + beta_ref[:, cols]
            a_ref[rs, cols] = (_silu(yn) * sz_ref[rs, cols].astype(F32)).astype(a_ref.dtype)
            return carry

        lax.fori_loop(0, n_cb, out_body, 0, unroll=4)


def _conv_gate(u, sz, dw_w, dw_b, ln_g, ln_b, tm=256):
    e = CONV_WIDTH
    w_pad = jnp.zeros((32, e), F32).at[:CONV_KERNEL].set(dw_w)
    hb = tm // CONV_HALO
    row = lambda v: v.reshape(1, e)
    return pl.pallas_call(
        functools.partial(_conv_kernel, tm=tm),
        grid=(SEQ // tm,),
        in_specs=[pl.BlockSpec((tm, e), lambda i: (i, 0)),
                  pl.BlockSpec((CONV_HALO, e), lambda i: (jnp.maximum(i * hb - 1, 0), 0)),
                  pl.BlockSpec((tm, e), lambda i: (i, 0)),
                  pl.BlockSpec((32, e), lambda i: (0, 0)),
                  pl.BlockSpec((1, e), lambda i: (0, 0)),
                  pl.BlockSpec((1, e), lambda i: (0, 0)),
                  pl.BlockSpec((1, e), lambda i: (0, 0))],
        out_specs=pl.BlockSpec((tm, e), lambda i: (i, 0)),
        out_shape=jax.ShapeDtypeStruct((SEQ, e), BF16),
        scratch_shapes=[pltpu.VMEM((SUBLANES, CONV_HALO + tm, LANES), F32),
                        pltpu.VMEM((tm, e), F32),
                        pltpu.VMEM((tm, LANES), F32)],
        compiler_params=_params("parallel"),
        name="conv_gate",
    )(u, u, sz, w_pad, row(dw_b), row(ln_g), row(ln_b))


def _out_ple_kernel(a_ref, wout_ref, x_ref, p_ref, gw_ref, pw_ref, gn_ref, *out_refs, final):
    y = x_ref[...] + jnp.dot(a_ref[...], wout_ref[...], preferred_element_type=F32)
    gate = _sigmoid(jnp.dot(y.astype(BF16), gw_ref[...], preferred_element_type=F32))
    pp = jnp.dot(p_ref[...].astype(BF16), pw_ref[...], preferred_element_type=F32)
    xn = y + gate * pp
    ms = jnp.mean(xn * xn, axis=-1, keepdims=True)
    hn = xn * lax.rsqrt(ms + EPS) * gn_ref[...]
    if final:
        out_refs[0][...] = hn
    else:
        out_refs[0][...] = xn
        out_refs[1][...] = hn.astype(out_refs[1].dtype)


def _out_ple(a, w_out, x, p_all, layer, gate_w, proj_w, g_next, final, tm=256):
    d = D_MODEL
    blk = pl.BlockSpec((tm, d), lambda i: (i, 0))
    if final:
        out_specs = [blk]
        out_shape = [jax.ShapeDtypeStruct((SEQ, d), F32)]
    else:
        out_specs = [blk, blk]
        out_shape = [jax.ShapeDtypeStruct((SEQ, d), F32), jax.ShapeDtypeStruct((SEQ, d), BF16)]
    return pl.pallas_call(
        functools.partial(_out_ple_kernel, final=final),
        grid=(SEQ // tm,),
        in_specs=[pl.BlockSpec((tm, VALUE_DIM), lambda i: (i, 0)),
                  _resident((VALUE_DIM, d)),
                  blk,
                  pl.BlockSpec((None, None, tm, PLE_DIM), lambda i: (layer, 0, i, 0)),
                  _resident((d, d)),
                  _resident((PLE_DIM, d)),
                  pl.BlockSpec((1, d), lambda i: (0, 0))],
        out_specs=out_specs,
        out_shape=out_shape,
        compiler_params=_params("parallel"),
        name="out_ple",
    )(a, w_out, x, p_all, gate_w, proj_w, g_next.reshape(1, d))


def _dproj_kernel(h_ref, w_ref, cw_ref, *refs, mode, tm, tn):
    if mode == "k":
        out_ref, kt_ref, wb_ref, ext_ref = refs
    elif mode == "z":
        out_ref, wb_ref = refs
    else:
        out_ref, wb_ref, ext_ref = refs

    @pl.when(pl.program_id(1) == 0)
    def _():
        wb_ref[...] = w_ref[...].astype(BF16)
        if mode != "z":
            ext_ref[:, 0:SUBLANES, :] = jnp.zeros((tn // MXU_WIDTH, SUBLANES, MXU_WIDTH), F32)

    h = h_ref[...]
    for s in range(tn // MXU_WIDTH):
        cols = slice(s * MXU_WIDTH, (s + 1) * MXU_WIDTH)
        y = jnp.dot(h, wb_ref[:, cols], preferred_element_type=F32)
        if mode == "z":
            out_ref[:, cols] = y.astype(out_ref.dtype)
            continue
        ext_ref[s, SUBLANES:SUBLANES + tm, :] = y
        acc = cw_ref[SHORT_CONV - 1:SHORT_CONV, cols] * y
        for k in range(SHORT_CONV - 1):
            start = SUBLANES - (SHORT_CONV - 1) + k
            acc = acc + cw_ref[k:k + 1, cols] * ext_ref[s, pl.ds(start, tm), :]
        ext_ref[s, 0:SUBLANES, :] = y[tm - SUBLANES:, :]
        act = _silu(acc)
        if mode == "v":
            out_ref[:, cols] = act.astype(out_ref.dtype)
            continue
        for hh in range(MXU_WIDTH // HEAD_DIM):
            hc = slice(s * MXU_WIDTH + hh * HEAD_DIM, s * MXU_WIDTH + (hh + 1) * HEAD_DIM)
            xh = act[:, hh * HEAD_DIM:(hh + 1) * HEAD_DIM]
            xn = xh * lax.rsqrt(jnp.sum(xh * xh, axis=-1, keepdims=True) + EPS)
            if mode == "q":
                xn = xn * (HEAD_DIM ** -0.5)
            out_ref[:, hc] = xn.astype(out_ref.dtype)
            if mode == "k":
                kt_ref[hc, :] = xn.T.astype(kt_ref.dtype)


def _delta_in_proj(h, w_in_all, layer, conv_w8, mode, tm=1024, tn=512):
    col0 = {"q": 0, "k": KEY_DIM, "v": 2 * KEY_DIM, "z": QKV_DIM}[mode]
    n = KEY_DIM if mode in ("q", "k") else VALUE_DIM
    off = col0 // tn
    cw_off = 0 if mode == "z" else off
    out_specs = [pl.BlockSpec((tm, tn), lambda j, i: (i, j))]
    out_shape = [jax.ShapeDtypeStruct((SEQ, n), BF16)]
    scratch = [pltpu.VMEM((D_MODEL, tn), BF16)]
    if mode == "k":
        out_specs.append(pl.BlockSpec((tn, tm), lambda j, i: (j, i)))
        out_shape.append(jax.ShapeDtypeStruct((n, SEQ), BF16))
    if mode != "z":
        scratch.append(pltpu.VMEM((tn // MXU_WIDTH, SUBLANES + tm, MXU_WIDTH), F32))
    return pl.pallas_call(
        functools.partial(_dproj_kernel, mode=mode, tm=tm, tn=tn),
        grid=(n // tn, SEQ // tm),
        in_specs=[pl.BlockSpec((tm, D_MODEL), lambda j, i: (i, 0)),
                  pl.BlockSpec((None, D_MODEL, tn), lambda j, i: (layer, 0, off + j)),
                  pl.BlockSpec((SUBLANES, tn), lambda j, i: (0, cw_off + j))],
        out_specs=out_specs,
        out_shape=out_shape,
        scratch_shapes=scratch,
        compiler_params=_params("parallel", "arbitrary"),
        name="delta_proj_" + mode,
    )(h, w_in_all, conv_w8)


def _delta_gates_kernel(h_ref, wba_ref, alog_ref, dtb_ref, s1_ref, s2_ref, *, tm):
    ba = jnp.dot(h_ref[...], wba_ref[...], preferred_element_type=F32)
    beta = _sigmoid(ba)
    xa = ba + dtb_ref[...]
    softplus = jnp.maximum(xa, 0.0) + jnp.log1p(jnp.exp(-jnp.abs(xa)))
    g = -jnp.exp(alog_ref[...]) * softplus
    pos = lax.broadcasted_iota(jnp.int32, (tm, LANES), 0) % CHUNK
    gc = g
    gs = g
    s = 1
    while s < CHUNK:
        gc = gc + jnp.where(pos >= s, pltpu.roll(gc, s, axis=0), 0.0)
        gs = gs + jnp.where(pos + s < CHUNK, pltpu.roll(gs, tm - s, axis=0), 0.0)
        s *= 2
    lane = lax.broadcasted_iota(jnp.int32, (tm, LANES), 1)
    s1_ref[...] = jnp.where(lane < N_V_HEADS, beta, gc)
    s2_ref[...] = gs - g


def _delta_gates(h, w_ba, a_log, dt_bias, tm=512):
    pad = lambda v: jnp.zeros((1, LANES), F32).at[0, N_V_HEADS:2 * N_V_HEADS].set(v)
    small = jax.ShapeDtypeStruct((SEQ, LANES), F32)
    return pl.pallas_call(
        functools.partial(_delta_gates_kernel, tm=tm),
        grid=(SEQ // tm,),
        in_specs=[pl.BlockSpec((tm, D_MODEL), lambda i: (i, 0)),
                  pl.BlockSpec((D_MODEL, LANES), lambda i: (0, 0)),
                  pl.BlockSpec((1, LANES), lambda i: (0, 0)),
                  pl.BlockSpec((1, LANES), lambda i: (0, 0))],
        out_specs=[pl.BlockSpec((tm, LANES), lambda i: (i, 0)),
                   pl.BlockSpec((tm, LANES), lambda i: (i, 0))],
        out_shape=[small, small],
        compiler_params=_params("parallel"),
        name="delta_gates",
    )(h, w_ba, pad(a_log), pad(dt_bias))


def _dot(a, b):
    return jnp.dot(a.astype(BF16), b.astype(BF16), preferred_element_type=F32)


def _delta_kernel(q_ref, k_ref, kt_ref, v_ref, z_ref, col_ref, row_ref, ng_ref, o_ref, s_ref, *, nc, npairs):
    @pl.when(pl.program_id(1) == 0)
    def _():
        s_ref[...] = jnp.zeros_like(s_ref)

    ii = lax.broadcasted_iota(jnp.int32, (CHUNK, CHUNK), 0)
    jj = lax.broadcasted_iota(jnp.int32, (CHUNK, CHUNK), 1)
    lower = ii >= jj
    strict = ii > jj
    eye = (ii == jj).astype(F32)
    blk_xor = ii ^ jj
    rows = lambda c: slice(c * CHUNK, (c + 1) * CHUNK)
    kcols = lambda pr: slice(pr * HEAD_DIM, (pr + 1) * HEAD_DIM)
    vcols = lambda pr, e: slice((2 * pr + e) * HEAD_DIM, (2 * pr + e + 1) * HEAD_DIM)
    units = [(c, pr, e) for c in range(nc) for pr in range(npairs) for e in range(2)]

    kk, qk0 = {}, {}
    for c in range(nc):
        for pr in range(npairs):
            ktc = kt_ref[kcols(pr), rows(c)]
            kk[c, pr] = _dot(k_ref[rows(c), kcols(pr)], ktc)
            qk0[c, pr] = _dot(q_ref[rows(c), kcols(pr)], ktc)

    dec, a, t = {}, {}, {}
    for un in units:
        c, pr, e = un
        bcol = col_ref[pr, rows(c), e:e + 1]
        gcol = col_ref[pr, rows(c), 2 + e:3 + e]
        grow = row_ref[pr, c, 2 + e:3 + e, :]
        dec[un] = jnp.where(lower, jnp.exp(jnp.where(lower, gcol - grow, 0.0)), 0.0)
        a[un] = jnp.where(strict, -(kk[c, pr] * bcol * dec[un]), 0.0)
        t[un] = eye + jnp.where(blk_xor < 2, a[un], 0.0)
    b = 2
    while b < CHUNK:
        coupling = (blk_xor >= b) & (blk_xor < 2 * b)
        x = {un: _dot(jnp.where(coupling, a[un], 0.0), t[un]) for un in units}
        for un in units:
            t[un] = t[un] + _dot(t[un], x[un])
        b *= 2
    uu, ww, qk = {}, {}, {}
    for un in units:
        c, pr, e = un
        tb = t[un] * row_ref[pr, c, e:e + 1, :]
        uu[un] = _dot(tb, v_ref[rows(c), vcols(pr, e)])
        ww[un] = _dot(tb * jnp.exp(row_ref[pr, c, 2 + e:3 + e, :]), k_ref[rows(c), kcols(pr)]).astype(BF16)
        qk[un] = (qk0[c, pr] * dec[un]).astype(BF16)

    heads = [(pr, e) for pr in range(npairs) for e in range(2)]
    for c in range(nc):
        sm, r = {}, {}
        for pr, e in heads:
            sm[pr, e] = s_ref[2 * pr + e]
            r[pr, e] = _dot(jnp.concatenate([ww[c, pr, e], q_ref[rows(c), kcols(pr)]], axis=0), sm[pr, e])
        for pr, e in heads:
            un = (c, pr, e)
            grrow = row_ref[pr, c, 4 + e:5 + e, :]
            gl = row_ref[pr, c, 2 + e:3 + e, :] + grrow
            vnew = (uu[un] - r[pr, e][:CHUNK]).astype(BF16)
            o = r[pr, e][CHUNK:] * jnp.exp(col_ref[pr, rows(c), 2 + e:3 + e]) + _dot(qk[un], vnew)
            kdt = kt_ref[kcols(pr), rows(c)].astype(F32) * jnp.exp(grrow)
            s_ref[2 * pr + e] = sm[pr, e] * jnp.exp(gl) + _dot(kdt, vnew)
            ms = jnp.mean(o * o, axis=-1, keepdims=True)
            on = o * lax.rsqrt(ms + EPS) * ng_ref[...]
            zc = z_ref[rows(c), vcols(pr, e)].astype(F32)
            o_ref[rows(c), vcols(pr, e)] = (on * _silu(zc)).astype(o_ref.dtype)


def _delta_rule(qn, kn, kt, vv, zz, col, row, norm_g, nc=2, npairs=4):
    tb = nc * CHUNK
    kw = npairs * HEAD_DIM
    vw = 2 * kw
    return pl.pallas_call(
        functools.partial(_delta_kernel, nc=nc, npairs=npairs),
        grid=(N_K_HEADS // npairs, SEQ // tb),
        in_specs=[pl.BlockSpec((tb, kw), lambda g, t: (t, g)),
                  pl.BlockSpec((tb, kw), lambda g, t: (t, g)),
                  pl.BlockSpec((kw, tb), lambda g, t: (g, t)),
                  pl.BlockSpec((tb, vw), lambda g, t: (t, g)),
                  pl.BlockSpec((tb, vw), lambda g, t: (t, g)),
                  pl.BlockSpec((npairs, tb, 8), lambda g, t: (g, t, 0)),
                  pl.BlockSpec((npairs, nc, 8, CHUNK), lambda g, t: (g, t, 0, 0)),
                  pl.BlockSpec((1, HEAD_DIM), lambda g, t: (0, 0))],
        out_specs=pl.BlockSpec((tb, vw), lambda g, t: (t, g)),
        out_shape=jax.ShapeDtypeStruct((SEQ, VALUE_DIM), BF16),
        scratch_shapes=[pltpu.VMEM((2 * npairs, HEAD_DIM, HEAD_DIM), F32)],
        compiler_params=_params("parallel", "arbitrary"),
        name="delta_rule",
    )(qn, kn, kt, vv, zz, col, row, norm_g.reshape(1, HEAD_DIM))


def _gate_layouts(s1, s2):
    pairs = lambda v: v.reshape(SEQ, N_K_HEADS, 2)
    beta = pairs(s1[:, :N_V_HEADS])
    gc = pairs(s1[:, N_V_HEADS:2 * N_V_HEADS])
    gr = pairs(s2[:, N_V_HEADS:2 * N_V_HEADS])
    col = jnp.concatenate([beta, gc, gr, jnp.zeros_like(beta)], axis=-1).transpose(1, 0, 2)
    row = col.reshape(N_K_HEADS, N_CHUNKS, CHUNK, 8).transpose(0, 1, 3, 2)
    return col, row


def kernel(x, p, conv_norm_g, conv_w_in, conv_dw_w, conv_dw_b, conv_ln_g, conv_ln_b, conv_w_out,
           delta_norm_g, delta_w_in, delta_conv_w, delta_a_log, delta_dt_bias, delta_out_norm_g,
           delta_w_out, ple_gate_w, ple_proj_w, final_norm_g):
    x = x.reshape(SEQ, D_MODEL)
    norm_gains = [conv_norm_g[0], delta_norm_g[0], conv_norm_g[1], delta_norm_g[1], final_norm_g]
    h = _rmsnorm(x, norm_gains[0])
    out = None
    for i in range(DEPTH):
        j = i // 2
        if i % 2 == 0:
            u, sz = _glu_proj(h, conv_w_in, j)
            a = _conv_gate(u, sz, conv_dw_w[j], conv_dw_b[j], conv_ln_g[j], conv_ln_b[j])
            w_out = conv_w_out[j]
        else:
            cw8 = jnp.zeros((SUBLANES, QKV_DIM), F32).at[:SHORT_CONV].set(delta_conv_w[j])
            qn, = _delta_in_proj(h, delta_w_in, j, cw8, "q")
            kn, kt = _delta_in_proj(h, delta_w_in, j, cw8, "k")
            vv, = _delta_in_proj(h, delta_w_in, j, cw8, "v")
            zz, = _delta_in_proj(h, delta_w_in, j, cw8, "z")
            w_ba = jnp.zeros((D_MODEL, LANES), BF16).at[:, :2 * N_V_HEADS].set(
                delta_w_in[j, :, QKV_DIM + VALUE_DIM:].astype(BF16))
            s1, s2 = _delta_gates(h, w_ba, delta_a_log[j], delta_dt_bias[j])
            col, row = _gate_layouts(s1, s2)
            a = _delta_rule(qn, kn, kt, vv, zz, col, row, delta_out_norm_g[j])
            w_out = delta_w_out[j]
        final = i == DEPTH - 1
        res = _out_ple(a, w_out.astype(BF16), x, p, i, ple_gate_w[i].astype(BF16),
                       ple_proj_w[i].astype(BF16), norm_gains[i + 1], final)
        if final:
            out = res[0]
        else:
            x, h = res
    return out.reshape(1, SEQ, D_MODEL)
```
